```python
import math
import jax, jax.numpy as jnp
from jax import lax
import numpy as np

D_MODEL = 2048
BATCH = 2
SEQ = 4096
DEPTH = 1
DEC_BATCH = 128
DEC_SEQ = 4
PAST_LEN = 16384
PAGE_SIZE = 128

N_META = 16
WINDOW = 128
BLOCK = 128
A_HEADS = 16
A_KV_HEADS = 4
A_GROUP = A_HEADS // A_KV_HEADS
A_HEAD_DIM = 64
A_ROT_DIM = A_HEAD_DIM // 4
A_ROPE_THETA = 500000.0
A_WIDTH = A_HEADS * A_HEAD_DIM
A_KV_WIDTH = A_KV_HEADS * A_HEAD_DIM
R_HEADS = 8
R_KEY_DIM = 128
R_VAL_DIM = 256
R_QK_WIDTH = R_HEADS * R_KEY_DIM
R_V_WIDTH = R_HEADS * R_VAL_DIM
R_ROPE_THETA = 10000.0
R_CHUNK = 128
EPS = 1e-6
GN_EPS = 1e-5
NEG_INF = -1e30
SPLITS = (A_WIDTH, A_KV_WIDTH, A_KV_WIDTH, A_WIDTH, R_QK_WIDTH, R_QK_WIDTH, R_V_WIDTH, R_V_WIDTH, D_MODEL, D_MODEL)
SPLIT_POINTS = tuple(int(s) for s in np.cumsum(SPLITS)[:-1])
IN_WIDTH = sum(SPLITS)

kernel_name = "hybrid_swa_sink_retention_meta_step"


def rms_norm(x, g):
    xf = x.astype(jnp.float32)
    y = xf * lax.rsqrt(jnp.mean(xf * xf, axis=-1, keepdims=True) + EPS) * g.astype(jnp.float32)
    return y.astype(x.dtype)


def rope(x, pos, rot_dim, theta):
    half = rot_dim // 2
    inv = jnp.exp(-math.log(theta) * 2.0 * jnp.arange(half, dtype=jnp.float32) / rot_dim)
    ang = pos.astype(jnp.float32)[:, None] * inv[None, :]
    cos = jnp.cos(ang)[:, None, :]
    sin = jnp.sin(ang)[:, None, :]
    xf = x[..., :rot_dim].astype(jnp.float32)
    x1, x2 = xf[..., :half], xf[..., half:]
    rot = jnp.concatenate([x1 * cos - x2 * sin, x2 * cos + x1 * sin], axis=-1).astype(x.dtype)
    return jnp.concatenate([rot, x[..., rot_dim:]], axis=-1)


def layer_inputs(h, pos, norm_g, w_in, q_g, k_g):
    b, t = h.shape[0], h.shape[1]
    u = jnp.einsum('btd,de->bte', rms_norm(h, norm_g), w_in)
    qa, ka, va, za, qr, kr, vr, zr, ga, gr = jnp.split(u, SPLIT_POINTS, axis=-1)
    qa = rope(rms_norm(qa.reshape(b, t, A_HEADS, A_HEAD_DIM), q_g), pos, A_ROT_DIM, A_ROPE_THETA)
    qa = qa.reshape(b, t, A_KV_HEADS, A_GROUP, A_HEAD_DIM)
    ka = rope(rms_norm(ka.reshape(b, t, A_KV_HEADS, A_HEAD_DIM), k_g), pos, A_ROT_DIM, A_ROPE_THETA)
    va = va.reshape(b, t, A_KV_HEADS, A_HEAD_DIM)
    qr = rope(qr.reshape(b, t, R_HEADS, R_KEY_DIM), pos, R_KEY_DIM, R_ROPE_THETA).astype(jnp.float32)
    kr = (rope(kr.reshape(b, t, R_HEADS, R_KEY_DIM), pos, R_KEY_DIM, R_ROPE_THETA).astype(jnp.float32)
          * (R_KEY_DIM ** -0.5))
    vr = vr.reshape(b, t, R_HEADS, R_VAL_DIM).astype(jnp.float32)
    return (qa, ka, va), (qr, kr, vr), (za, zr, ga, gr)


def sink_attend(q, k, v, mask, sink):
    s = jnp.einsum('...qhgd,...khd->...hgqk', q, k).astype(jnp.float32) * (A_HEAD_DIM ** -0.5)
    s = jnp.where(mask, s, NEG_INF)
    sk = sink.astype(jnp.float32)[:, :, None, None]
    m = jnp.maximum(jnp.max(s, axis=-1, keepdims=True), sk)
    p = jnp.exp(s - m)
    p = p / (jnp.sum(p, axis=-1, keepdims=True) + jnp.exp(sk - m))
    o = jnp.einsum('...hgqk,...khd->...qhgd', p.astype(v.dtype), v)
    return o.reshape(o.shape[:-3] + (A_WIDTH,))


def retention_chunk(S, q, k, v, lg):
    S = S.astype(jnp.float32)
    c = q.shape[1]
    idx = jnp.arange(c, dtype=jnp.float32)
    rel = idx[:, None] - idx[None, :]
    decay = jnp.where(rel >= 0, jnp.exp(jnp.maximum(rel, 0.0)[None] * lg[:, None, None]), 0.0)
    inner = jnp.einsum('bihd,bjhd->bhij', q, k) * decay
    o = jnp.einsum('bhij,bjhe->bihe', inner, v)
    o = o + jnp.einsum('bihd,bhde->bihe', q, S) * jnp.exp((idx[:, None] + 1.0) * lg[None, :])[..., None]
    wk = jnp.exp((c - 1.0 - idx)[:, None] * lg[None, :])
    S_new = jnp.exp(c * lg)[:, None, None] * S + jnp.einsum('bjhd,bjhe->bhde', k * wk[..., None], v)
    return o, S_new


def retention_scan(S0, q, k, v, lg):
    b, t, h = q.shape[0], q.shape[1], q.shape[2]
    n = t // R_CHUNK

    def to_chunks(a):
        return a.reshape(b, n, R_CHUNK, h, a.shape[-1]).swapaxes(0, 1)

    def step(S, qkv):
        o, S = retention_chunk(S, qkv[0], qkv[1], qkv[2], lg)
        return S, o

    S, o = lax.scan(step, S0, (to_chunks(q), to_chunks(k), to_chunks(v)))
    return o.swapaxes(0, 1).reshape(b, t, h, R_VAL_DIM), S


def merge_out(o_a, o_r, gates, gn_g, gn_b, w_pa, w_pr, w_o):
    za, zr, ga, gr = gates
    mu = jnp.mean(o_r, axis=-1, keepdims=True)
    var = jnp.mean(jnp.square(o_r - mu), axis=-1, keepdims=True)
    o_r = ((o_r - mu) * lax.rsqrt(var + GN_EPS)).reshape(o_r.shape[:-2] + (R_V_WIDTH,))
    o_r = (o_r * gn_g.astype(jnp.float32) + gn_b.astype(jnp.float32)).astype(zr.dtype)
    y_a = (o_a * jax.nn.silu(za)) @ w_pa
    y_r = (o_r * jax.nn.silu(zr)) @ w_pr
    return (jax.nn.sigmoid(ga) * y_a + jax.nn.sigmoid(gr) * y_r) @ w_o


def setup_inputs(seed: int = 0) -> dict:
    key = jax.random.key(seed)
    ks = jax.random.split(key, 18)
    win_buf = min(WINDOW, PAST_LEN)
    f = jnp.float32
    nrm = jax.random.normal
    return {
        "x_prompt": nrm(ks[0], (BATCH, SEQ, D_MODEL), f),
        "x_sample": nrm(ks[1], (DEC_BATCH, DEC_SEQ, D_MODEL), f),
        "cache_win_k": nrm(ks[2], (DEPTH, DEC_BATCH, win_buf, A_KV_HEADS, A_HEAD_DIM), f),
        "cache_win_v": nrm(ks[3], (DEPTH, DEC_BATCH, win_buf, A_KV_HEADS, A_HEAD_DIM), f),
        "state_ret": 0.5 * nrm(ks[4], (DEPTH, DEC_BATCH, R_HEADS, R_KEY_DIM, R_VAL_DIM), f),
        "meta_tokens": nrm(ks[5], (N_META, D_MODEL), f),
        "norm_gain": 1.0 + 0.02 * nrm(ks[6], (DEPTH, D_MODEL), f),
        "w_in": nrm(ks[7], (DEPTH, D_MODEL, IN_WIDTH), f) * D_MODEL ** -0.5,
        "q_norm_gain": 1.0 + 0.02 * nrm(ks[8], (DEPTH, A_HEAD_DIM), f),
        "k_norm_gain": 1.0 + 0.02 * nrm(ks[9], (DEPTH, A_HEAD_DIM), f),
        "attn_sinks": 0.5 * nrm(ks[10], (DEPTH, A_HEADS), f),
        "ret_gn_gain": 1.0 + 0.02 * nrm(ks[11], (DEPTH, R_V_WIDTH), f),
        "ret_gn_bias": 0.02 * nrm(ks[12], (DEPTH, R_V_WIDTH), f),
        "w_branch_attn": nrm(ks[13], (DEPTH, A_WIDTH, D_MODEL), f) * A_WIDTH ** -0.5,
        "w_branch_ret": nrm(ks[14], (DEPTH, R_V_WIDTH, D_MODEL), f) * R_V_WIDTH ** -0.5,
        "w_out": nrm(ks[15], (DEPTH, D_MODEL, D_MODEL), f) * D_MODEL ** -0.5,
    }


def reference(x_prompt, x_sample, cache_win_k, cache_win_v, state_ret, meta_tokens, norm_gain, w_in,
              q_norm_gain, k_norm_gain, attn_sinks, ret_gn_gain, ret_gn_bias, w_branch_attn,
              w_branch_ret, w_out):
    lg = jnp.asarray(np.log(1.0 - np.exp(np.linspace(np.log(1.0 / 32), np.log(1.0 / 512), R_HEADS)))
                     .astype(np.float32))
    b_p, t_p = x_prompt.shape[0], x_prompt.shape[1]
    b_s, t_s = x_sample.shape[0], x_sample.shape[1]
    win_buf = cache_win_k.shape[2]
    nb = t_p // BLOCK

    pos_m = jnp.arange(N_META)
    pos_p = N_META + jnp.arange(t_p)
    pos_s = PAST_LEN + jnp.arange(t_s)

    mask_m = jnp.tril(jnp.ones((N_META, N_META), bool))
    qi = jnp.arange(BLOCK)[:, None]
    kj = jnp.arange(2 * BLOCK)[None, :] - BLOCK
    band = (kj <= qi) & (kj > qi - WINDOW)
    band = band[None] & ((jnp.arange(nb)[:, None, None] * BLOCK + kj[None]) >= 0)
    mask_p = jnp.concatenate([jnp.ones((nb, BLOCK, N_META), bool), band], axis=-1)[None, :, None, None]
    kpos = jnp.concatenate([PAST_LEN - win_buf + jnp.arange(win_buf), pos_s])
    band_s = ((kpos[None] <= pos_s[:, None]) & (kpos[None] > pos_s[:, None] - WINDOW)
              & (kpos[None] >= N_META))
    mask_s = jnp.concatenate([jnp.ones((t_s, N_META), bool), band_s], axis=-1)

    h_m = meta_tokens[None].astype(x_prompt.dtype)
    h_p = x_prompt
    h_s = x_sample
    wk_p, wv_p, rs_p, wk_s, wv_s, rs_s = [], [], [], [], [], []
    for l in range(DEPTH):
        sink = attn_sinks[l].reshape(A_KV_HEADS, A_GROUP)
        lw = (norm_gain[l], w_in[l], q_norm_gain[l], k_norm_gain[l])
        mw = (ret_gn_gain[l], ret_gn_bias[l], w_branch_attn[l], w_branch_ret[l], w_out[l])

        (qm, km, vm), (rqm, rkm, rvm), gm = layer_inputs(h_m, pos_m, *lw)
        o_am = sink_attend(qm, km, vm, mask_m, sink)
        o_rm, S_m = retention_chunk(jnp.zeros((1, R_HEADS, R_KEY_DIM, R_VAL_DIM), jnp.float32),
                                    rqm, rkm, rvm, lg)
        h_m_next = h_m + merge_out(o_am, o_rm, gm, *mw)

        (qp, kp, vp), (rqp, rkp, rvp), gp = layer_inputs(h_p, pos_p, *lw)
        qb = qp.reshape(b_p, nb, BLOCK, A_KV_HEADS, A_GROUP, A_HEAD_DIM)
        kb = kp.reshape(b_p, nb, BLOCK, A_KV_HEADS, A_HEAD_DIM)
        vb = vp.reshape(b_p, nb, BLOCK, A_KV_HEADS, A_HEAD_DIM)
        shp = (b_p, nb, N_META, A_KV_HEADS, A_HEAD_DIM)
        k_all = jnp.concatenate([jnp.broadcast_to(km[:, None], shp).astype(kb.dtype),
                                 jnp.concatenate([jnp.zeros_like(kb[:, :1]), kb[:, :-1]], axis=1), kb], axis=2)
        v_all = jnp.concatenate([jnp.broadcast_to(vm[:, None], shp).astype(vb.dtype),
                                 jnp.concatenate([jnp.zeros_like(vb[:, :1]), vb[:, :-1]], axis=1), vb], axis=2)
        o_ap = sink_attend(qb, k_all, v_all, mask_p, sink).reshape(b_p, t_p, A_WIDTH)
        S0 = jnp.broadcast_to(S_m, (b_p, R_HEADS, R_KEY_DIM, R_VAL_DIM))
        o_rp, S_p = retention_scan(S0, rqp, rkp, rvp, lg)
        h_p = h_p + merge_out(o_ap, o_rp, gp, *mw)
        wk_p.append(kp[:, -win_buf:])
        wv_p.append(vp[:, -win_buf:])
        rs_p.append(S_p)

        (qs, ks_, vs), (rqs, rks, rvs), gs = layer_inputs(h_s, pos_s, *lw)
        kw = jnp.concatenate([cache_win_k[l].astype(ks_.dtype), ks_], axis=1)
        vw = jnp.concatenate([cache_win_v[l].astype(vs.dtype), vs], axis=1)
        shs = (b_s, N_META, A_KV_HEADS, A_HEAD_DIM)
        k_all_s = jnp.concatenate([jnp.broadcast_to(km[0], shs).astype(kw.dtype), kw], axis=1)
        v_all_s = jnp.concatenate([jnp.broadcast_to(vm[0], shs).astype(vw.dtype), vw], axis=1)
        o_as = sink_attend(qs, k_all_s, v_all_s, mask_s, sink)
        o_rs, S_s = retention_chunk(state_ret[l], rqs, rks, rvs, lg)
        h_s = h_s + merge_out(o_as, o_rs, gs, *mw)
        wk_s.append(kw[:, -win_buf:])
        wv_s.append(vw[:, -win_buf:])
        rs_s.append(S_s)

        h_m = h_m_next

    win_k_prompt = jnp.stack(wk_p)
    win_v_prompt = jnp.stack(wv_p)
    ret_prompt = jnp.stack(rs_p)
    win_k_sample = jnp.stack(wk_s)
    win_v_sample = jnp.stack(wv_s)
    ret_sample = jnp.stack(rs_s)
    return (h_p, h_s, win_k_prompt, win_v_prompt, ret_prompt, win_k_sample, win_v_sample, ret_sample)
```

```python
import functools
import math

import jax
import jax.numpy as jnp
import numpy as np
from jax import lax
from jax.experimental import pallas as pl
from jax.experimental.pallas import tpu as pltpu

F32 = jnp.float32
BF16 = jnp.bfloat16

D_MODEL = 2048
N_META = 16
WINDOW = 128
BLOCK = 128
A_HEADS = 16
A_KV_HEADS = 4
A_GROUP = A_HEADS // A_KV_HEADS
A_HEAD_DIM = 64
A_ROT_DIM = A_HEAD_DIM // 4
A_ROPE_THETA = 500000.0
A_WIDTH = A_HEADS * A_HEAD_DIM
A_KV_WIDTH = A_KV_HEADS * A_HEAD_DIM
R_HEADS = 8
R_KEY_DIM = 128
R_VAL_DIM = 256
R_QK_WIDTH = R_HEADS * R_KEY_DIM
R_V_WIDTH = R_HEADS * R_VAL_DIM
R_ROPE_THETA = 10000.0
R_CHUNK = 128
EPS = 1e-6
GN_EPS = 1e-5
NEG_INF = -1e30
SPLITS = (A_WIDTH, A_KV_WIDTH, A_KV_WIDTH, A_WIDTH, R_QK_WIDTH, R_QK_WIDTH, R_V_WIDTH, R_V_WIDTH,
          D_MODEL, D_MODEL)
IN_WIDTH = sum(SPLITS)
O_QA, O_KA, O_VA, O_ZA, O_QR, O_KR, O_VR, O_ZR, O_GA, O_GR = (
    int(v) for v in np.concatenate([[0], np.cumsum(SPLITS)[:-1]]))

LANES = 128
SUBLANES = 8
PAIR = 2
KEY_PAD = 3 * BLOCK
VMEM_LIMIT = 56 * 1024 * 1024

_NT = (((1,), (1,)), ((), ()))
_TN = (((0,), (0,)), ((), ()))


def _dot(a, b):
  return jnp.dot(a, b, preferred_element_type=F32)


def _dot_nt(a, b):
  return lax.dot_general(a, b, _NT, preferred_element_type=F32)


def _dot_tn(a, b):
  return lax.dot_general(a, b, _TN, preferred_element_type=F32)


def _head_rms(x, e, gain):
  xx = x * x
  hi = xx.astype(BF16)
  lo = (xx - hi.astype(F32)).astype(BF16)
  ss = _dot(hi, e) + _dot(lo, e)
  return x * lax.rsqrt(ss * (1.0 / A_HEAD_DIM) + EPS) * gain


def _rope_a(x, c, s1, s2):
  return x * c + pltpu.roll(x, LANES - A_ROT_DIM // 2, 1) * s1 + pltpu.roll(x, A_ROT_DIM // 2, 1) * s2


def _norm_rope_a(x, e, gain, c, s1, s2):
  n = _head_rms(x, e, gain)
  return jnp.concatenate(
      [_rope_a(n[:, :LANES], c, s1, s2), _rope_a(n[:, LANES:], c, s1, s2)], axis=1)


def _rope_r(x, c, s):
  return x * c + pltpu.roll(x, R_KEY_DIM // 2, 1) * s


def _dup_half(col, odd, lo):
  rolled = pltpu.roll(col, A_HEAD_DIM, 1)
  if odd:
    return jnp.where(lo, rolled, col)
  return jnp.where(lo, col, rolled)


def _gn_gate(o, gng, gnb, z):
  mu = jnp.mean(o, axis=-1, keepdims=True)
  d = o - mu
  var = jnp.mean(d * d, axis=-1, keepdims=True)
  on = d * lax.rsqrt(var + GN_EPS)
  on = on * gng + gnb
  return (on * (z * jax.nn.sigmoid(z))).astype(BF16)


def _merge_store(u_ref, x_ref, ga_ref, gr_ref, wpa_ref, wpr_ref, wo_ref, y_ref):
  ya = _dot(ga_ref[...], wpa_ref[...])
  yr = _dot(gr_ref[...], wpr_ref[...])
  m = (jax.nn.sigmoid(u_ref[:, O_GA:O_GA + D_MODEL]) * ya
       + jax.nn.sigmoid(u_ref[:, O_GR:O_GR + D_MODEL]) * yr)
  y_ref[...] = x_ref[...] + _dot(m.astype(BF16), wo_ref[...])


def _inproj_kernel(x_ref, g_ref, w_ref, o_ref, xn_ref):
  @pl.when(pl.program_id(1) == 0)
  def _():
    x = x_ref[...]
    ms = jnp.mean(x * x, axis=-1, keepdims=True)
    xn_ref[...] = (x * lax.rsqrt(ms + EPS) * g_ref[...]).astype(BF16)

  o_ref[...] = _dot(xn_ref[...], w_ref[...])


def _inproj(x, gain, w_bf, tm, tn=1280):
  rows = x.shape[0]
  return pl.pallas_call(
      _inproj_kernel,
      grid=(rows // tm, IN_WIDTH // tn),
      in_specs=[
          pl.BlockSpec((tm, D_MODEL), lambda i, j: (i, 0)),
          pl.BlockSpec((1, D_MODEL), lambda i, j: (0, 0)),
          pl.BlockSpec((D_MODEL, tn), lambda i, j: (0, j)),
      ],
      out_specs=pl.BlockSpec((tm, tn), lambda i, j: (i, j)),
      out_shape=jax.ShapeDtypeStruct((rows, IN_WIDTH), F32),
      scratch_shapes=[pltpu.VMEM((tm, D_MODEL), BF16)],
      compiler_params=pltpu.CompilerParams(
          dimension_semantics=("arbitrary", "arbitrary"), vmem_limit_bytes=VMEM_LIMIT),
      name="inproj",
  )(x, gain, w_bf)


def _prep_kernel(u_ref, ca_ref, s1_ref, s2_ref, cr_ref, sr_ref, qgain_ref, kgain_ref, e_ref, sel_ref,
                 qsel_ref, kn_ref, va_ref, qr_ref, kr_ref, vr_ref):
  e = e_ref[...]
  ca, s1, s2 = ca_ref[...], s1_ref[...], s2_ref[...]
  cr, sr = cr_ref[...], sr_ref[...]
  qn = jnp.concatenate(
      [_norm_rope_a(u_ref[:, O_QA + j * 256:O_QA + (j + 1) * 256], e, qgain_ref[...], ca, s1, s2)
       for j in range(A_WIDTH // 256)], axis=1).astype(BF16)
  for g in range(A_GROUP):
    qsel_ref[g] = _dot(qn, sel_ref[g])
  kn_ref[...] = _norm_rope_a(u_ref[:, O_KA:O_KA + A_KV_WIDTH], e, kgain_ref[...], ca, s1, s2)
  va_ref[...] = u_ref[:, O_VA:O_VA + A_KV_WIDTH]
  for h in range(R_HEADS):
    cols = slice(h * R_KEY_DIM, (h + 1) * R_KEY_DIM)
    qr_ref[:, cols] = _rope_r(u_ref[:, O_QR + h * R_KEY_DIM:O_QR + (h + 1) * R_KEY_DIM], cr, sr)
    kr_ref[:, cols] = (_rope_r(u_ref[:, O_KR + h * R_KEY_DIM:O_KR + (h + 1) * R_KEY_DIM], cr, sr)
                       * (R_KEY_DIM ** -0.5))
  vr_ref[...] = u_ref[:, O_VR:O_VR + R_V_WIDTH]


def _prep(u, tabs, qgain, kgain, e, sel, tr):
  rows = u.shape[0]
  row = lambda w: pl.BlockSpec((tr, w), lambda i: (i, 0))
  full = lambda a: pl.BlockSpec(a.shape, lambda i: (0,) * a.ndim)
  return pl.pallas_call(
      _prep_kernel,
      grid=(rows // tr,),
      in_specs=[row(IN_WIDTH)] + [row(LANES)] * 5 + [full(qgain), full(kgain), full(e), full(sel)],
      out_specs=[
          pl.BlockSpec((A_GROUP, tr, A_KV_WIDTH), lambda i: (0, i, 0)),
          row(A_KV_WIDTH), row(A_KV_WIDTH), row(R_QK_WIDTH), row(R_QK_WIDTH), row(R_V_WIDTH)],
      out_shape=[
          jax.ShapeDtypeStruct((A_GROUP, rows, A_KV_WIDTH), F32),
          jax.ShapeDtypeStruct((rows, A_KV_WIDTH), F32),
          jax.ShapeDtypeStruct((rows, A_KV_WIDTH), F32),
          jax.ShapeDtypeStruct((rows, R_QK_WIDTH), F32),
          jax.ShapeDtypeStruct((rows, R_QK_WIDTH), F32),
          jax.ShapeDtypeStruct((rows, R_V_WIDTH), F32)],
      compiler_params=pltpu.CompilerParams(
          dimension_semantics=("arbitrary",), vmem_limit_bytes=VMEM_LIMIT),
      name="prep",
  )(u, *tabs, qgain, kgain, e, sel)


def _prompt_kernel(sink_ref, gam_ref,
                   u_ref, x_ref, ca_ref, s1_ref, s2_ref, cr_ref, sr_ref, mask_ref,
                   qgain_ref, kgain_ref, e_ref,
                   km_ref, vm_ref, krm_ref, vrm_ref, wkm_ref,
                   dec_ref, rowdec_ref, wk_ref, gng_ref, gnb_ref,
                   wpa_ref, wpr_ref, wo_ref,
                   y_ref, wko_ref, wvo_ref, so_ref,
                   k2_ref, v2_ref, s_ref, ga_ref, gr_ref):
  c = pl.program_id(1)
  lo = lax.broadcasted_iota(jnp.int32, (BLOCK, LANES), 1) < A_HEAD_DIM

  @pl.when(c == 0)
  def _init():
    lo_m = lax.broadcasted_iota(jnp.int32, (N_META, LANES), 1) < A_HEAD_DIM
    pad = jnp.zeros((BLOCK - N_META, LANES), F32)
    for kvh in range(A_KV_HEADS):
      cols = slice((kvh // 2) * LANES, (kvh // 2 + 1) * LANES)
      k2_ref[kvh, 0:BLOCK, :] = jnp.zeros((BLOCK, LANES), BF16)
      v2_ref[kvh, 0:BLOCK, :] = jnp.zeros((BLOCK, LANES), BF16)
      k2_ref[kvh, 2 * BLOCK:KEY_PAD, :] = jnp.concatenate(
          [_dup_half(km_ref[:, cols], kvh % 2, lo_m), pad], axis=0).astype(BF16)
      v2_ref[kvh, 2 * BLOCK:KEY_PAD, :] = jnp.concatenate(
          [_dup_half(vm_ref[:, cols], kvh % 2, lo_m), pad], axis=0).astype(BF16)
    for h in range(R_HEADS):
      kw = (krm_ref[:, h * R_KEY_DIM:(h + 1) * R_KEY_DIM] * wkm_ref[h]).astype(BF16)
      s_ref[h] = _dot_tn(kw, vrm_ref[:, h * R_VAL_DIM:(h + 1) * R_VAL_DIM].astype(BF16))

  e = e_ref[...]
  ca, s1, s2 = ca_ref[...], s1_ref[...], s2_ref[...]
  cr, sr = cr_ref[...], sr_ref[...]

  kn = _norm_rope_a(u_ref[:, O_KA:O_KA + A_KV_WIDTH], e, kgain_ref[...], ca, s1, s2)
  va = u_ref[:, O_VA:O_VA + A_KV_WIDTH]
  for kvh in range(A_KV_HEADS):
    cols = slice((kvh // 2) * LANES, (kvh // 2 + 1) * LANES)
    k2_ref[kvh, BLOCK:2 * BLOCK, :] = _dup_half(kn[:, cols], kvh % 2, lo).astype(BF16)
    v2_ref[kvh, BLOCK:2 * BLOCK, :] = _dup_half(va[:, cols], kvh % 2, lo).astype(BF16)

  @pl.when(c == pl.num_programs(1) - 1)
  def _window_out():
    wko_ref[0] = kn
    wvo_ref[0] = va

  visible = mask_ref[...] > 0.5
  for kvh in range(A_KV_HEADS):
    qn = _norm_rope_a(u_ref[:, O_QA + kvh * 256:O_QA + (kvh + 1) * 256], e, qgain_ref[...],
                      ca, s1, s2)
    qc = (qn[:, :LANES], qn[:, LANES:])
    lhs = jnp.concatenate(
        [jnp.where(lo, qc[0], 0.0), jnp.where(lo, 0.0, qc[0]),
         jnp.where(lo, qc[1], 0.0), jnp.where(lo, 0.0, qc[1])], axis=0).astype(BF16)
    s = _dot_nt(lhs, k2_ref[kvh]) * (A_HEAD_DIM ** -0.5)
    ps, dens = [], []
    for g in range(A_GROUP):
      sg = jnp.where(visible, s[g * BLOCK:(g + 1) * BLOCK], NEG_INF)
      sk = sink_ref[kvh * A_GROUP + g]
      m = jnp.maximum(jnp.max(sg, axis=-1, keepdims=True), sk)
      p = jnp.exp(sg - m)
      dens.append(jnp.sum(p, axis=-1, keepdims=True) + jnp.exp(sk - m))
      ps.append(p.astype(BF16))
    pv = _dot(jnp.concatenate(ps, axis=0), v2_ref[kvh])
    og = [pv[g * BLOCK:(g + 1) * BLOCK] / dens[g] for g in range(A_GROUP)]
    for j in range(2):
      col = 2 * kvh + j
      o = jnp.where(lo, og[2 * j], og[2 * j + 1])
      z = u_ref[:, O_ZA + col * LANES:O_ZA + (col + 1) * LANES]
      ga_ref[:, col * LANES:(col + 1) * LANES] = (o * (z * jax.nn.sigmoid(z))).astype(BF16)

  for kvh in range(A_KV_HEADS):
    k2_ref[kvh, 0:BLOCK, :] = k2_ref[kvh, BLOCK:2 * BLOCK, :]
    v2_ref[kvh, 0:BLOCK, :] = v2_ref[kvh, BLOCK:2 * BLOCK, :]

  for h in range(R_HEADS):
    q = _rope_r(u_ref[:, O_QR + h * R_KEY_DIM:O_QR + (h + 1) * R_KEY_DIM], cr, sr)
    k = _rope_r(u_ref[:, O_KR + h * R_KEY_DIM:O_KR + (h + 1) * R_KEY_DIM], cr, sr) * (R_KEY_DIM ** -0.5)
    v = u_ref[:, O_VR + h * R_VAL_DIM:O_VR + (h + 1) * R_VAL_DIM].astype(BF16)
    qb = q.astype(BF16)
    st = s_ref[h]
    inner = _dot_nt(qb, k.astype(BF16)) * dec_ref[h]
    o = _dot(inner.astype(BF16), v) + _dot(qb, st.astype(BF16)) * rowdec_ref[h]
    s_ref[h] = gam_ref[h] * st + _dot_tn((k * wk_ref[h]).astype(BF16), v)
    cols = slice(h * R_VAL_DIM, (h + 1) * R_VAL_DIM)
    gr_ref[:, cols] = _gn_gate(o, gng_ref[:, cols], gnb_ref[:, cols],
                               u_ref[:, O_ZR + h * R_VAL_DIM:O_ZR + (h + 1) * R_VAL_DIM])

  @pl.when(c == pl.num_programs(1) - 1)
  def _state_out():
    so_ref[0] = s_ref[...]

  _merge_store(u_ref, x_ref, ga_ref, gr_ref, wpa_ref, wpr_ref, wo_ref, y_ref)


def _const_spec(a, ngrid):
  return pl.BlockSpec(a.shape, lambda *_: (0,) * a.ndim, pipeline_mode=pl.Buffered(1))


def _prompt_mix(u, x, tabs, mask2, sinks, gam, qgain, kgain, e, meta, decs, gng, gnb, wpa, wpr, wo,
                batch, seq):
  nb = seq // BLOCK
  smem = pl.BlockSpec(memory_space=pltpu.SMEM)
  rowblk = lambda w: pl.BlockSpec((BLOCK, w), lambda b, c: (b * nb + c, 0))
  tabblk = pl.BlockSpec((BLOCK, LANES), lambda b, c: (c, 0))
  consts = [qgain, kgain, e, *meta, *decs, gng, gnb, wpa, wpr, wo]
  return pl.pallas_call(
      _prompt_kernel,
      grid=(batch, nb),
      in_specs=[smem, smem, rowblk(IN_WIDTH), rowblk(D_MODEL)] + [tabblk] * 5
      + [pl.BlockSpec((None, BLOCK, KEY_PAD), lambda b, c: (jnp.minimum(c, 1), 0, 0))]
      + [_const_spec(a, 2) for a in consts],
      out_specs=[
          rowblk(D_MODEL),
          pl.BlockSpec((1, WINDOW, A_KV_WIDTH), lambda b, c: (b, 0, 0)),
          pl.BlockSpec((1, WINDOW, A_KV_WIDTH), lambda b, c: (b, 0, 0)),
          pl.BlockSpec((1, R_HEADS, R_KEY_DIM, R_VAL_DIM), lambda b, c: (b, 0, 0, 0))],
      out_shape=[
          jax.ShapeDtypeStruct((batch * seq, D_MODEL), F32),
          jax.ShapeDtypeStruct((batch, WINDOW, A_KV_WIDTH), F32),
          jax.ShapeDtypeStruct((batch, WINDOW, A_KV_WIDTH), F32),
          jax.ShapeDtypeStruct((batch, R_HEADS, R_KEY_DIM, R_VAL_DIM), F32)],
      scratch_shapes=[
          pltpu.VMEM((A_KV_HEADS, KEY_PAD, LANES), BF16),
          pltpu.VMEM((A_KV_HEADS, KEY_PAD, LANES), BF16),
          pltpu.VMEM((R_HEADS, R_KEY_DIM, R_VAL_DIM), F32),
          pltpu.VMEM((BLOCK, A_WIDTH), BF16),
          pltpu.VMEM((BLOCK, R_V_WIDTH), BF16)],
      compiler_params=pltpu.CompilerParams(
          dimension_semantics=("arbitrary", "arbitrary"), vmem_limit_bytes=VMEM_LIMIT),
      name="prompt_mix",
  )(sinks, gam, u, x, *tabs, mask2, *consts)


def _sample_kernel(gam_ref,
                   qsel_ref, kn_ref, va_ref, qr_ref, kr_ref, vr_ref,
                   ck_ref, cv_ref, st_ref,
                   km_ref, vm_ref, mask_ref, sink_ref, d8_ref, rowdec_ref, wk_ref,
                   og_ref, or_ref, cko_ref, cvo_ref, sto_ref):
  rows = A_HEADS * SUBLANES
  dec_seq = SUBLANES // PAIR
  lane_kvh = lax.shift_right_logical(
      lax.broadcasted_iota(jnp.int32, (SUBLANES, A_KV_WIDTH), 1), 6)
  row_seq = lax.shift_right_logical(lax.broadcasted_iota(jnp.int32, (rows, 1), 0), 2) & (PAIR - 1)
  r8_seq = lax.shift_right_logical(lax.broadcasted_iota(jnp.int32, (SUBLANES, 1), 0), 2)

  pieces = []
  for g in range(A_GROUP):
    qg = qsel_ref[g]
    for kvh in range(A_KV_HEADS):
      pieces.append(jnp.where(lane_kvh == kvh, qg, 0.0))
  big = jnp.concatenate(pieces, axis=0).astype(BF16)
  sc = [_dot_nt(big, ck_ref[bi].astype(BF16)) for bi in range(PAIR)]
  s_cache = jnp.where(row_seq == 0, sc[0], sc[1])
  zpad = jnp.zeros((WINDOW - SUBLANES - N_META, A_KV_WIDTH), F32)
  kx = jnp.concatenate([kn_ref[...], km_ref[...], zpad], axis=0).astype(BF16)
  vx = jnp.concatenate([va_ref[...], vm_ref[...], zpad], axis=0).astype(BF16)
  s = jnp.concatenate([s_cache, _dot_nt(big, kx)], axis=1) * (A_HEAD_DIM ** -0.5)
  s = jnp.where(mask_ref[...] > 0.5, s, NEG_INF)
  sk = sink_ref[...]
  m = jnp.maximum(jnp.max(s, axis=-1, keepdims=True), sk)
  p = jnp.exp(s - m)
  den = jnp.sum(p, axis=-1, keepdims=True) + jnp.exp(sk - m)
  pc = p[:, :WINDOW]
  o = _dot(p[:, WINDOW:].astype(BF16), vx)
  for bi in range(PAIR):
    o = o + _dot(jnp.where(row_seq == bi, pc, 0.0).astype(BF16), cv_ref[bi].astype(BF16))
  o = o / den
  for g in range(A_GROUP):
    acc = jnp.zeros((SUBLANES, A_KV_WIDTH), F32)
    for kvh in range(A_KV_HEADS):
      r0 = (g * A_KV_HEADS + kvh) * SUBLANES
      acc = acc + jnp.where(lane_kvh == kvh, o[r0:r0 + SUBLANES], 0.0)
    og_ref[g] = acc

  for bi in range(PAIR):
    cko_ref[bi, 0:WINDOW - dec_seq, :] = ck_ref[bi, dec_seq:WINDOW, :]
    cko_ref[bi, WINDOW - dec_seq:WINDOW, :] = kn_ref[bi * dec_seq:(bi + 1) * dec_seq, :]
    cvo_ref[bi, 0:WINDOW - dec_seq, :] = cv_ref[bi, dec_seq:WINDOW, :]
    cvo_ref[bi, WINDOW - dec_seq:WINDOW, :] = va_ref[bi * dec_seq:(bi + 1) * dec_seq, :]

  for h in range(R_HEADS):
    kcols = slice(h * R_KEY_DIM, (h + 1) * R_KEY_DIM)
    vcols = slice(h * R_VAL_DIM, (h + 1) * R_VAL_DIM)
    qb = qr_ref[:, kcols].astype(BF16)
    kf = kr_ref[:, kcols]
    v = vr_ref[:, vcols].astype(BF16)
    inner = _dot_nt(qb, kf.astype(BF16)) * d8_ref[h]
    o = _dot(inner.astype(BF16), v)
    kwk = kf * wk_ref[h]
    for bi in range(PAIR):
      st = st_ref[bi, h]
      o = o + jnp.where(r8_seq == bi, _dot(qb, st.astype(BF16)) * rowdec_ref[h], 0.0)
      kw = jnp.where(r8_seq == bi, kwk, 0.0).astype(BF16)
      sto_ref[bi, h] = gam_ref[h] * st + _dot_tn(kw, v)
    or_ref[:, vcols] = o


def _sample_mix(gam4, prep, ck, cv, st, km, vm, mask, sinkrows, d8, rowdec8, wk8):
  qsel, kn, va, qr, kr, vr = prep
  nseq = ck.shape[0]
  rows = nseq * (SUBLANES // PAIR)
  smem = pl.BlockSpec(memory_space=pltpu.SMEM)
  row = lambda w: pl.BlockSpec((SUBLANES, w), lambda i: (i, 0))
  full = lambda a: pl.BlockSpec(a.shape, lambda i: (0,) * a.ndim)
  cache = pl.BlockSpec((PAIR, WINDOW, A_KV_WIDTH), lambda i: (i, 0, 0))
  state = pl.BlockSpec((PAIR, R_HEADS, R_KEY_DIM, R_VAL_DIM), lambda i: (i, 0, 0, 0))
  qsel_spec = pl.BlockSpec((A_GROUP, SUBLANES, A_KV_WIDTH), lambda i: (0, i, 0))
  return pl.pallas_call(
      _sample_kernel,
      grid=(nseq // PAIR,),
      in_specs=[smem, qsel_spec, row(A_KV_WIDTH), row(A_KV_WIDTH), row(R_QK_WIDTH), row(R_QK_WIDTH),
                row(R_V_WIDTH), cache, cache, state,
                full(km), full(vm), full(mask), full(sinkrows), full(d8), full(rowdec8), full(wk8)],
      out_specs=[qsel_spec, row(R_V_WIDTH), cache, cache, state],
      out_shape=[
          jax.ShapeDtypeStruct((A_GROUP, rows, A_KV_WIDTH), F32),
          jax.ShapeDtypeStruct((rows, R_V_WIDTH), F32),
          jax.ShapeDtypeStruct(ck.shape, F32),
          jax.ShapeDtypeStruct(cv.shape, F32),
          jax.ShapeDtypeStruct(st.shape, F32)],
      compiler_params=pltpu.CompilerParams(
          dimension_semantics=("arbitrary",), vmem_limit_bytes=VMEM_LIMIT),
      name="sample_mix",
  )(gam4, qsel, kn, va, qr, kr, vr, ck, cv, st, km, vm, mask, sinkrows, d8, rowdec8, wk8)


def _sample_merge_kernel(u_ref, x_ref, oa_ref, or_ref, gng_ref, gnb_ref, wpa_ref, wpr_ref, wo_ref,
                         y_ref, ga_ref, gr_ref):
  za = u_ref[:, O_ZA:O_ZA + A_WIDTH]
  ga_ref[...] = (oa_ref[...] * (za * jax.nn.sigmoid(za))).astype(BF16)
  for h in range(R_HEADS):
    cols = slice(h * R_VAL_DIM, (h + 1) * R_VAL_DIM)
    gr_ref[:, cols] = _gn_gate(or_ref[:, cols], gng_ref[:, cols], gnb_ref[:, cols],
                               u_ref[:, O_ZR + h * R_VAL_DIM:O_ZR + (h + 1) * R_VAL_DIM])
  _merge_store(u_ref, x_ref, ga_ref, gr_ref, wpa_ref, wpr_ref, wo_ref, y_ref)


def _sample_merge(u, x, oa, o_r, gng, gnb, wpa, wpr, wo):
  rows = u.shape[0]
  row = lambda w: pl.BlockSpec((BLOCK, w), lambda i: (i, 0))
  consts = [gng, gnb, wpa, wpr, wo]
  return pl.pallas_call(
      _sample_merge_kernel,
      grid=(rows // BLOCK,),
      in_specs=[row(IN_WIDTH), row(D_MODEL), row(A_WIDTH), row(R_V_WIDTH)]
      + [_const_spec(a, 1) for a in consts],
      out_specs=row(D_MODEL),
      out_shape=jax.ShapeDtypeStruct((rows, D_MODEL), F32),
      scratch_shapes=[pltpu.VMEM((BLOCK, A_WIDTH), BF16), pltpu.VMEM((BLOCK, R_V_WIDTH), BF16)],
      compiler_params=pltpu.CompilerParams(
          dimension_semantics=("arbitrary",), vmem_limit_bytes=VMEM_LIMIT),
      name="sample_merge",
  )(u, x, oa, o_r, *consts)


def _attn_rope_tables(pos):
  half = A_ROT_DIM // 2
  inv = jnp.exp(-math.log(A_ROPE_THETA) * 2.0 * jnp.arange(half, dtype=F32) / A_ROT_DIM)
  ang = pos.astype(F32)[:, None] * inv[None, :]
  cos, sin = jnp.cos(ang), jnp.sin(ang)
  t = pos.shape[0]
  rest = A_HEAD_DIM - A_ROT_DIM
  z8, zr, one = jnp.zeros((t, half), F32), jnp.zeros((t, rest), F32), jnp.ones((t, rest), F32)
  c = jnp.concatenate([cos, cos, one], axis=1)
  s1 = jnp.concatenate([-sin, z8, zr], axis=1)
  s2 = jnp.concatenate([z8, sin, zr], axis=1)
  return tuple(jnp.concatenate([a, a], axis=1) for a in (c, s1, s2))


def _ret_rope_tables(pos):
  half = R_KEY_DIM // 2
  inv = jnp.exp(-math.log(R_ROPE_THETA) * 2.0 * jnp.arange(half, dtype=F32) / R_KEY_DIM)
  ang = pos.astype(F32)[:, None] * inv[None, :]
  cos, sin = jnp.cos(ang), jnp.sin(ang)
  return jnp.concatenate([cos, cos], axis=1), jnp.concatenate([-sin, sin], axis=1)


def _tables(pos):
  return (*_attn_rope_tables(pos), *_ret_rope_tables(pos))


def _decay_tables(lg, c):
  idx = jnp.arange(c, dtype=F32)
  rel = idx[:, None] - idx[None, :]
  decay = jnp.where(rel >= 0, jnp.exp(jnp.maximum(rel, 0.0)[None] * lg[:, None, None]), 0.0)
  rowdec = jnp.exp((idx[:, None] + 1.0) * lg[None, :]).T
  wk = jnp.exp((c - 1.0 - idx)[:, None] * lg[None, :]).T
  return decay, rowdec, wk, jnp.exp(c * lg)


def _head_sum_matrix():
  h = np.arange(256) // A_HEAD_DIM
  return jnp.asarray(h[:, None] == h[None, :], BF16)


def _group_select_matrices():
  sel = np.zeros((A_GROUP, A_WIDTH, A_KV_WIDTH), np.float32)
  d = np.arange(A_HEAD_DIM)
  for g in range(A_GROUP):
    for kvh in range(A_KV_HEADS):
      sel[g, (kvh * A_GROUP + g) * A_HEAD_DIM + d, kvh * A_HEAD_DIM + d] = 1.0
  return jnp.asarray(sel, BF16)


def _prompt_masks():
  qi = np.arange(BLOCK)[:, None]
  kj = np.arange(2 * BLOCK)[None, :] - BLOCK
  band = (kj <= qi) & (kj > qi - WINDOW)
  out = np.zeros((2, BLOCK, KEY_PAD), np.float32)
  for first in (0, 1):
    blk = 0 if first == 0 else 1
    b = band & ((blk * BLOCK + kj) >= 0)
    out[first, :, :2 * BLOCK] = b
    out[first, :, 2 * BLOCK:2 * BLOCK + N_META] = 1.0
  return jnp.asarray(out)


def _sample_mask(past_len, dec_seq, win_buf):
  rows = A_HEADS * SUBLANES
  r = np.arange(rows)
  t = r % dec_seq
  seq = (r // dec_seq) % PAIR
  pos = past_len + t
  kpos = past_len - win_buf + np.arange(win_buf)
  cache = ((kpos[None] <= pos[:, None]) & (kpos[None] > pos[:, None] - WINDOW)
           & (kpos[None] >= N_META))
  j = np.arange(SUBLANES)
  npos = past_len + (j % dec_seq)
  new = ((npos[None] <= pos[:, None]) & (npos[None] > pos[:, None] - WINDOW)
         & (npos[None] >= N_META) & ((j // dec_seq)[None] == seq[:, None]))
  out = np.zeros((rows, 2 * WINDOW), np.float32)
  out[:, :win_buf] = cache
  out[:, WINDOW:WINDOW + SUBLANES] = new
  out[:, WINDOW + SUBLANES:WINDOW + SUBLANES + N_META] = 1.0
  return jnp.asarray(out)


def kernel(x_prompt, x_sample, cache_win_k, cache_win_v, state_ret, meta_tokens, norm_gain, w_in,
           q_norm_gain, k_norm_gain, attn_sinks, ret_gn_gain, ret_gn_bias, w_branch_attn,
           w_branch_ret, w_out):
  b_p, t_p = x_prompt.shape[0], x_prompt.shape[1]
  b_s, t_s = x_sample.shape[0], x_sample.shape[1]
  win_buf = cache_win_k.shape[2]
  past_len = 16384
  assert cache_win_k.shape[0] == 1 and t_s * PAIR == SUBLANES and win_buf == WINDOW
  assert t_p % BLOCK == 0 and b_s % PAIR == 0

  lg = jnp.asarray(np.log(1.0 - np.exp(np.linspace(np.log(1.0 / 32), np.log(1.0 / 512), R_HEADS)))
                   .astype(np.float32))
  w_in_bf = w_in[0].astype(BF16)
  wpa, wpr, wo = (w_branch_attn[0].astype(BF16), w_branch_ret[0].astype(BF16), w_out[0].astype(BF16))
  ng = norm_gain[0].reshape(1, D_MODEL)
  qgain = jnp.tile(q_norm_gain[0], 256 // A_HEAD_DIM).reshape(1, 256)
  kgain = jnp.tile(k_norm_gain[0], 256 // A_HEAD_DIM).reshape(1, 256)
  gng = ret_gn_gain[0].reshape(1, R_V_WIDTH)
  gnb = ret_gn_bias[0].reshape(1, R_V_WIDTH)
  sinks = attn_sinks[0]
  e = _head_sum_matrix()
  sel = _group_select_matrices()

  xp = x_prompt.reshape(b_p * t_p, D_MODEL)
  xs = x_sample.reshape(b_s * t_s, D_MODEL)

  u_p = _inproj(xp, ng, w_in_bf, tm=1024)
  u_s = _inproj(xs, ng, w_in_bf, tm=b_s * t_s)
  u_m = _inproj(meta_tokens.astype(x_prompt.dtype), ng, w_in_bf, tm=N_META)

  tabs_m = _tables(jnp.arange(N_META))
  _, km, vm, _, krm, vrm = _prep(u_m, tabs_m, qgain, kgain, e, sel, tr=N_META)
  _, _, wk_m, _ = _decay_tables(lg, N_META)
  wkm = jnp.broadcast_to(wk_m[:, :, None], (R_HEADS, N_META, R_KEY_DIM))

  tabs_p = _tables(N_META + jnp.arange(t_p))
  dec, rowdec, wk, gam = _decay_tables(lg, R_CHUNK)
  decs = (dec,
          jnp.broadcast_to(rowdec[:, :, None], (R_HEADS, R_CHUNK, R_VAL_DIM)),
          jnp.broadcast_to(wk[:, :, None], (R_HEADS, R_CHUNK, R_KEY_DIM)))
  y_p, wk_p, wv_p, s_p = _prompt_mix(
      u_p, xp, tabs_p, _prompt_masks(), sinks, gam, qgain, kgain, e, (km, vm, krm, vrm, wkm), decs,
      gng, gnb, wpa, wpr, wo, b_p, t_p)

  pos_s = past_len + jnp.arange(t_s)
  tabs_s = tuple(jnp.tile(a, (b_s, 1)) for a in _tables(pos_s))
  prep_s = _prep(u_s, tabs_s, qgain, kgain, e, sel, tr=BLOCK)
  dec4, rowdec4, wk4, gam4 = _decay_tables(lg, t_s)
  d8 = jnp.zeros((R_HEADS, SUBLANES, SUBLANES), F32)
  for i in range(PAIR):
    d8 = d8.at[:, i * t_s:(i + 1) * t_s, i * t_s:(i + 1) * t_s].set(dec4)
  rowdec8 = jnp.broadcast_to(jnp.tile(rowdec4, (1, PAIR))[:, :, None], (R_HEADS, SUBLANES, R_VAL_DIM))
  wk8 = jnp.broadcast_to(jnp.tile(wk4, (1, PAIR))[:, :, None], (R_HEADS, SUBLANES, R_KEY_DIM))
  sinkrows = jnp.repeat(sinks.reshape(A_KV_HEADS, A_GROUP).T.reshape(-1), SUBLANES).reshape(-1, 1)
  og, o_r, ck_new, cv_new, st_new = _sample_mix(
      gam4, prep_s,
      cache_win_k[0].reshape(b_s, win_buf, A_KV_WIDTH), cache_win_v[0].reshape(b_s, win_buf, A_KV_WIDTH),
      state_ret[0], km, vm, _sample_mask(past_len, t_s, win_buf), sinkrows, d8, rowdec8, wk8)
  oa_s = og.reshape(A_GROUP, b_s * t_s, A_KV_HEADS, A_HEAD_DIM).transpose(1, 2, 0, 3).reshape(
      b_s * t_s, A_WIDTH)
  y_s = _sample_merge(u_s, xs, oa_s, o_r, gng, gnb, wpa, wpr, wo)

  return (y_p.reshape(b_p, t_p, D_MODEL),
          y_s.reshape(b_s, t_s, D_MODEL),
          wk_p.reshape(1, b_p, WINDOW, A_KV_HEADS, A_HEAD_DIM),
          wv_p.reshape(1, b_p, WINDOW, A_KV_HEADS, A_HEAD_DIM),
          s_p[None],
          ck_new.reshape(1, b_s, win_buf, A_KV_HEADS, A_HEAD_DIM),
          cv_new.reshape(1, b_s, win_buf, A_KV_HEADS, A_HEAD_DIM),
          st_new[None])
```
